```python
import jax, jax.numpy as jnp
from jax import lax
import numpy as np

D_MODEL = 2048
BATCH = 1
SEQ = 8192
DEPTH = 2
DEC_BATCH = 8
DEC_SEQ = 64
PAST_LEN = 1024

CHUNK = 64
D_CONV = D_MODEL // 2
SC_WIDTH = 3
SSD_HEAD_DIM = 64
D_INNER = D_MODEL
SSD_HEADS = D_INNER // SSD_HEAD_DIM
SSD_GROUPS = 8
SSD_STATE = 128
SSD_CONV_WIDTH = 4
D_XBC = D_INNER + 2 * SSD_GROUPS * SSD_STATE
D_MIX = D_CONV + D_INNER
D_IN_PROJ = 3 * D_CONV + D_INNER + D_XBC + SSD_HEADS
D_FF = 11 * D_MODEL // 4
EPS = 1e-6

kernel_name = "hymba_shortconv_ssd_macaron_stream_step"


def rmsnorm(x, g):
    xf = x.astype(jnp.float32)
    y = xf * lax.rsqrt(jnp.mean(xf * xf, axis=-1, keepdims=True) + EPS)
    return (y * g.astype(jnp.float32)).astype(x.dtype)


def swiglu(h, w_gate, w_up, w_down):
    return (jax.nn.silu(h @ w_gate) * (h @ w_up)) @ w_down


def causal_dwconv(u, hist, w, bias=None):
    width = w.shape[0]
    T = u.shape[1]
    full = jnp.concatenate([hist.astype(u.dtype), u], axis=1)
    y = full[:, 0:T] * w[0]
    for k in range(1, width):
        y = y + full[:, k:k + T] * w[k]
    if bias is not None:
        y = y + bias
    return y, full[:, T:]


def ssd_scan(x, dt, A, Bm, Cm, init_state):
    b, T, H, P = x.shape
    G, N = Bm.shape[2], Bm.shape[3]
    R = H // G
    Q = min(CHUNK, T)
    nc = T // Q
    f32 = jnp.float32
    xs = (x.astype(f32) * dt[..., None]).reshape(b, nc, Q, G, R, P)
    dA = (dt * A).reshape(b, nc, Q, G, R)
    Bc = Bm.astype(f32).reshape(b, nc, Q, G, N)
    Cc = Cm.astype(f32).reshape(b, nc, Q, G, N)
    Acum = jnp.cumsum(dA, axis=2)
    seg = Acum[:, :, :, None] - Acum[:, :, None, :]
    mask = jnp.tril(jnp.ones((Q, Q), dtype=bool))[:, :, None, None]
    Lmat = jnp.exp(jnp.where(mask, seg, -jnp.inf))
    CB = jnp.einsum("bclgn,bcsgn->bclsg", Cc, Bc)
    y_diag = jnp.einsum("bclsgr,bcsgrp->bclgrp", CB[..., None] * Lmat, xs)
    decay_to_end = jnp.exp(Acum[:, :, -1:] - Acum)
    blk_states = jnp.einsum("bclgn,bclgrp->bcgrpn", Bc, xs * decay_to_end[..., None])
    blk_decay = jnp.exp(Acum[:, :, -1])

    def step(s, inp):
        dec, add = inp
        return s * dec[..., None, None] + add, s

    s0 = init_state.astype(f32).reshape(b, G, R, P, N)
    final, prev = lax.scan(step, s0, (jnp.moveaxis(blk_decay, 1, 0), jnp.moveaxis(blk_states, 1, 0)))
    prev = jnp.moveaxis(prev, 0, 1)
    y_off = jnp.einsum("bclgn,bcgrpn->bclgrp", Cc, prev) * jnp.exp(Acum)[..., None]
    y = (y_diag + y_off).reshape(b, T, H, P)
    return y, final.reshape(b, H, P, N).astype(init_state.dtype)


def gated_group_rmsnorm(y, z, w):
    b, T, _ = y.shape
    u = y.astype(jnp.float32) * jax.nn.silu(z.astype(jnp.float32))
    u = u.reshape(b, T, SSD_GROUPS, D_INNER // SSD_GROUPS)
    u = u * lax.rsqrt(jnp.mean(u * u, axis=-1, keepdims=True) + EPS)
    return (u.reshape(b, T, D_INNER) * w.astype(jnp.float32)).astype(y.dtype)


def hybrid_mixer(h, sc_hist, xbc_hist, ssm_state, p, l):
    b, T, _ = h.shape
    proj = h @ p["w_in_mix"][l]
    splits = [D_CONV, 2 * D_CONV, 3 * D_CONV, 3 * D_CONV + D_INNER, 3 * D_CONV + D_INNER + D_XBC]
    sc_b, sc_c, sc_x, z, xbc, dt_raw = jnp.split(proj, splits, axis=-1)
    conv_a, new_sc = causal_dwconv(sc_c * sc_x, sc_hist, p["sc_conv_w"][l])
    y_a = sc_b * conv_a
    xbc_c, new_xbc = causal_dwconv(xbc, xbc_hist, p["ssd_conv_w"][l], p["ssd_conv_b"][l])
    xbc_c = jax.nn.silu(xbc_c)
    xs, Bm, Cm = jnp.split(xbc_c, [D_INNER, D_INNER + SSD_GROUPS * SSD_STATE], axis=-1)
    xs = xs.reshape(b, T, SSD_HEADS, SSD_HEAD_DIM)
    Bm = Bm.reshape(b, T, SSD_GROUPS, SSD_STATE)
    Cm = Cm.reshape(b, T, SSD_GROUPS, SSD_STATE)
    dt = jax.nn.softplus(dt_raw.astype(jnp.float32) + p["ssd_dt_bias"][l].astype(jnp.float32))
    A = -jnp.exp(p["ssd_A_log"][l].astype(jnp.float32))
    y, new_ssm = ssd_scan(xs, dt, A, Bm, Cm, ssm_state)
    y = y + xs.astype(jnp.float32) * p["ssd_D"][l].astype(jnp.float32)[:, None]
    y_b = gated_group_rmsnorm(y.reshape(b, T, D_INNER).astype(h.dtype), z, p["ssd_norm"][l])
    out = jnp.concatenate([y_a, y_b], axis=-1) @ p["w_out_mix"][l]
    return out, new_sc, new_xbc, new_ssm


def run_trunk(x, sc_hist, xbc_hist, ssm, p):
    new_sc, new_xbc, new_ssm = [], [], []
    for l in range(DEPTH):
        h = rmsnorm(x, p["ffn1_norm"][l])
        x = x + 0.5 * swiglu(h, p["ffn1_w_gate"][l], p["ffn1_w_up"][l], p["ffn1_w_down"][l])
        mix, s_sc, s_xbc, s_ssm = hybrid_mixer(rmsnorm(x, p["mix_norm"][l]), sc_hist[l], xbc_hist[l], ssm[l], p, l)
        x = x + mix
        h = rmsnorm(x, p["ffn2_norm"][l])
        x = x + 0.5 * swiglu(h, p["ffn2_w_gate"][l], p["ffn2_w_up"][l], p["ffn2_w_down"][l])
        new_sc.append(s_sc)
        new_xbc.append(s_xbc)
        new_ssm.append(s_ssm)
    return rmsnorm(x, p["final_norm"]), jnp.stack(new_sc), jnp.stack(new_xbc), jnp.stack(new_ssm)


def setup_inputs(seed: int = 0) -> dict:
    key = jax.random.key(seed)
    ks = jax.random.split(key, 24)
    f32 = jnp.float32
    nrm = lambda k, shape, s: jax.random.normal(k, shape, f32) * s
    dt0 = jnp.exp(jax.random.uniform(ks[10], (DEPTH, SSD_HEADS), f32) * (jnp.log(0.1) - jnp.log(0.001)) + jnp.log(0.001))
    return {
        "x_prompt": nrm(ks[0], (BATCH, SEQ, D_MODEL), 1.0),
        "x_sample": nrm(ks[1], (DEC_BATCH, DEC_SEQ, D_MODEL), 1.0),
        "state_sc_conv": nrm(ks[2], (DEPTH, DEC_BATCH, SC_WIDTH - 1, D_CONV), 1.0),
        "state_ssd_conv": nrm(ks[3], (DEPTH, DEC_BATCH, SSD_CONV_WIDTH - 1, D_XBC), 1.0),
        "state_ssm": nrm(ks[4], (DEPTH, DEC_BATCH, SSD_HEADS, SSD_HEAD_DIM, SSD_STATE), 0.5),
        "ffn1_norm": 1.0 + nrm(ks[5], (DEPTH, D_MODEL), 0.02),
        "ffn1_w_gate": nrm(ks[6], (DEPTH, D_MODEL, D_FF), D_MODEL ** -0.5),
        "ffn1_w_up": nrm(ks[7], (DEPTH, D_MODEL, D_FF), D_MODEL ** -0.5),
        "ffn1_w_down": nrm(ks[8], (DEPTH, D_FF, D_MODEL), D_FF ** -0.5),
        "mix_norm": 1.0 + nrm(ks[9], (DEPTH, D_MODEL), 0.02),
        "w_in_mix": nrm(ks[11], (DEPTH, D_MODEL, D_IN_PROJ), D_MODEL ** -0.5),
        "sc_conv_w": nrm(ks[12], (DEPTH, SC_WIDTH, D_CONV), SC_WIDTH ** -0.5),
        "ssd_conv_w": nrm(ks[13], (DEPTH, SSD_CONV_WIDTH, D_XBC), SSD_CONV_WIDTH ** -0.5),
        "ssd_conv_b": nrm(ks[14], (DEPTH, D_XBC), 0.02),
        "ssd_dt_bias": dt0 + jnp.log(-jnp.expm1(-dt0)),
        "ssd_A_log": jnp.log(jax.random.uniform(ks[15], (DEPTH, SSD_HEADS), f32, 1.0, 16.0)),
        "ssd_D": 1.0 + nrm(ks[16], (DEPTH, SSD_HEADS), 0.02),
        "ssd_norm": 1.0 + nrm(ks[17], (DEPTH, D_INNER), 0.02),
        "w_out_mix": nrm(ks[18], (DEPTH, D_MIX, D_MODEL), D_MIX ** -0.5),
        "ffn2_norm": 1.0 + nrm(ks[19], (DEPTH, D_MODEL), 0.02),
        "ffn2_w_gate": nrm(ks[20], (DEPTH, D_MODEL, D_FF), D_MODEL ** -0.5),
        "ffn2_w_up": nrm(ks[21], (DEPTH, D_MODEL, D_FF), D_MODEL ** -0.5),
        "ffn2_w_down": nrm(ks[22], (DEPTH, D_FF, D_MODEL), D_FF ** -0.5),
        "final_norm": 1.0 + nrm(ks[23], (D_MODEL,), 0.02),
    }


def reference(x_prompt, x_sample, state_sc_conv, state_ssd_conv, state_ssm,
              ffn1_norm, ffn1_w_gate, ffn1_w_up, ffn1_w_down, mix_norm, w_in_mix,
              sc_conv_w, ssd_conv_w, ssd_conv_b, ssd_dt_bias, ssd_A_log, ssd_D, ssd_norm,
              w_out_mix, ffn2_norm, ffn2_w_gate, ffn2_w_up, ffn2_w_down, final_norm):
    p = dict(ffn1_norm=ffn1_norm, ffn1_w_gate=ffn1_w_gate, ffn1_w_up=ffn1_w_up, ffn1_w_down=ffn1_w_down,
             mix_norm=mix_norm, w_in_mix=w_in_mix, sc_conv_w=sc_conv_w, ssd_conv_w=ssd_conv_w,
             ssd_conv_b=ssd_conv_b, ssd_dt_bias=ssd_dt_bias, ssd_A_log=ssd_A_log, ssd_D=ssd_D,
             ssd_norm=ssd_norm, w_out_mix=w_out_mix, ffn2_norm=ffn2_norm, ffn2_w_gate=ffn2_w_gate,
             ffn2_w_up=ffn2_w_up, ffn2_w_down=ffn2_w_down, final_norm=final_norm)
    b0 = x_prompt.shape[0]
    dt_ = x_prompt.dtype
    zero_sc = jnp.zeros((DEPTH, b0, SC_WIDTH - 1, D_CONV), dt_)
    zero_xbc = jnp.zeros((DEPTH, b0, SSD_CONV_WIDTH - 1, D_XBC), dt_)
    zero_ssm = jnp.zeros((DEPTH, b0, SSD_HEADS, SSD_HEAD_DIM, SSD_STATE), dt_)
    y_prompt, p_sc, p_xbc, p_ssm = run_trunk(x_prompt, zero_sc, zero_xbc, zero_ssm, p)
    y_sample, s_sc, s_xbc, s_ssm = run_trunk(x_sample, state_sc_conv, state_ssd_conv, state_ssm, p)
    return (y_prompt, y_sample, p_sc, p_xbc, p_ssm, s_sc, s_xbc, s_ssm)
```

```python
import functools

import jax
import jax.numpy as jnp
from jax import lax
from jax.experimental import pallas as pl
from jax.experimental.pallas import tpu as pltpu

F32 = jnp.float32
BF16 = jnp.bfloat16

EPS = 1e-6
CHUNK = 64
HEAD_DIM = 64
HEADS = 32
GROUPS = 8
STATE = 128
HEADS_PER_GROUP = HEADS // GROUPS
D_CONV = 1024
D_INNER = HEADS * HEAD_DIM
D_BC = GROUPS * STATE
D_XBC = D_INNER + 2 * D_BC
D_A = 3 * D_CONV
D_PROJ = D_XBC + D_INNER + D_A
D_MIX = D_CONV + D_INNER
NORM_GROUP = D_INNER // GROUPS
LANES = 128
HIST_ROWS = 8
VMEM_LIMIT = 56 * 1024 * 1024


def _rmsnorm(x, g):
    return x * lax.rsqrt(jnp.mean(x * x, axis=-1, keepdims=True) + EPS) * g


def _silu(x):
    return x * jax.nn.sigmoid(x)


def _ffn_kernel(x_ref, g_ref, wg_ref, wu_ref, wd_ref, fin_ref, o_ref, h_ref, acc_ref,
                *, apply_final):
    f = pl.program_id(1)
    nf = pl.num_programs(1)

    @pl.when(f == 0)
    def _():
        h_ref[...] = _rmsnorm(x_ref[...], g_ref[...]).astype(BF16)

    h = h_ref[...]
    gate = jnp.dot(h, wg_ref[...], preferred_element_type=F32)
    up = jnp.dot(h, wu_ref[...], preferred_element_type=F32)
    act = (_silu(gate) * up).astype(BF16)
    part = jnp.dot(act, wd_ref[...], preferred_element_type=F32)

    @pl.when(f == 0)
    def _():
        acc_ref[...] = part

    @pl.when(f > 0)
    def _():
        acc_ref[...] += part

    @pl.when(f == nf - 1)
    def _():
        y = x_ref[...] + 0.5 * acc_ref[...]
        if apply_final:
            y = _rmsnorm(y, fin_ref[...])
        o_ref[...] = y


def _ffn(x, g, wg, wu, wd, fin, *, apply_final, tm, tf):
    m, d = x.shape
    dff = wg.shape[1]
    return pl.pallas_call(
        functools.partial(_ffn_kernel, apply_final=apply_final),
        grid=(m // tm, dff // tf),
        in_specs=[
            pl.BlockSpec((tm, d), lambda i, f: (i, 0)),
            pl.BlockSpec((1, d), lambda i, f: (0, 0)),
            pl.BlockSpec((d, tf), lambda i, f: (0, f)),
            pl.BlockSpec((d, tf), lambda i, f: (0, f)),
            pl.BlockSpec((tf, d), lambda i, f: (f, 0)),
            pl.BlockSpec((1, d), lambda i, f: (0, 0)),
        ],
        out_specs=pl.BlockSpec((tm, d), lambda i, f: (i, 0)),
        out_shape=jax.ShapeDtypeStruct((m, d), F32),
        scratch_shapes=[pltpu.VMEM((tm, d), BF16), pltpu.VMEM((tm, d), F32)],
        compiler_params=pltpu.CompilerParams(
            dimension_semantics=("parallel", "arbitrary"), vmem_limit_bytes=VMEM_LIMIT),
        name="ffn",
    )(x, g, wg, wu, wd, fin)


def _inproj_kernel(x_ref, g_ref, w_ref, wdt_ref, o_ref, dt_ref, h_ref):
    n = pl.program_id(1)

    @pl.when(n == 0)
    def _():
        h = _rmsnorm(x_ref[...], g_ref[...]).astype(BF16)
        h_ref[...] = h
        dt_ref[...] = jnp.dot(h, wdt_ref[...], preferred_element_type=F32)

    o_ref[...] = jnp.dot(h_ref[...], w_ref[...], preferred_element_type=F32)


def _inproj(x, g, w, wdt, *, tm, tn):
    m, d = x.shape
    n = w.shape[1]
    return pl.pallas_call(
        _inproj_kernel,
        grid=(m // tm, n // tn),
        in_specs=[
            pl.BlockSpec((tm, d), lambda i, j: (i, 0)),
            pl.BlockSpec((1, d), lambda i, j: (0, 0)),
            pl.BlockSpec((d, tn), lambda i, j: (0, j)),
            pl.BlockSpec((d, LANES), lambda i, j: (0, 0)),
        ],
        out_specs=[
            pl.BlockSpec((tm, tn), lambda i, j: (i, j)),
            pl.BlockSpec((tm, LANES), lambda i, j: (i, 0)),
        ],
        out_shape=[jax.ShapeDtypeStruct((m, n), F32), jax.ShapeDtypeStruct((m, LANES), F32)],
        scratch_shapes=[pltpu.VMEM((tm, d), BF16)],
        compiler_params=pltpu.CompilerParams(
            dimension_semantics=("parallel", "arbitrary"), vmem_limit_bytes=VMEM_LIMIT),
        name="inproj",
    )(x, g, w, wdt)


def _outproj_kernel(x_ref, y_ref, w_ref, o_ref):
    o_ref[...] = x_ref[...] + jnp.dot(y_ref[...], w_ref[...], preferred_element_type=F32)


def _outproj(x, y, w, *, tm, tn):
    m, d = x.shape
    k = y.shape[1]
    return pl.pallas_call(
        _outproj_kernel,
        grid=(m // tm, d // tn),
        in_specs=[
            pl.BlockSpec((tm, tn), lambda i, j: (i, j)),
            pl.BlockSpec((tm, k), lambda i, j: (i, 0)),
            pl.BlockSpec((k, tn), lambda i, j: (0, j)),
        ],
        out_specs=pl.BlockSpec((tm, tn), lambda i, j: (i, j)),
        out_shape=jax.ShapeDtypeStruct((m, d), F32),
        compiler_params=pltpu.CompilerParams(
            dimension_semantics=("parallel", "arbitrary"), vmem_limit_bytes=VMEM_LIMIT),
        name="outproj",
    )(x, y, w)


def _softplus(x):
    return jnp.maximum(x, 0.0) + jnp.log1p(jnp.exp(-jnp.abs(x)))


def _split3(f):
    hi = f.astype(BF16)
    r1 = f - hi.astype(F32)
    mid = r1.astype(BF16)
    lo = (r1 - mid.astype(F32)).astype(BF16)
    return [hi, mid, lo]


def _mixer_kernel(xbc_ref, z_ref, a_ref, dtr_ref, sch_ref, xh_ref, st0_ref,
                  scw_ref, cw_ref, cb_ref, dtb_ref, alog_ref, dexp_ref, nw_ref, e_ref,
                  y_ref, nsc_ref, nxbc_ref, st_ref,
                  abuf, xbuf, *, prompt_chunks):
    q = CHUNK
    i = pl.program_id(0)
    first = jnp.logical_or(i == 0, i >= prompt_chunks)

    @pl.when(first)
    def _():
        abuf[HIST_ROWS - 2:HIST_ROWS, :] = sch_ref[0]
        xbuf[HIST_ROWS - 3:HIST_ROWS, :] = xh_ref[0]
        st_ref[...] = st0_ref[...]

    a = a_ref[...]
    cx = a[:, D_CONV:2 * D_CONV] * a[:, 2 * D_CONV:]
    abuf[HIST_ROWS:, :] = cx
    scw = scw_ref[...]
    conv_a = (abuf[HIST_ROWS - 2:HIST_ROWS - 2 + q, :] * scw[0:1]
              + abuf[HIST_ROWS - 1:HIST_ROWS - 1 + q, :] * scw[1:2]
              + cx * scw[2:3])
    y_ref[:, :D_CONV] = (a[:, :D_CONV] * conv_a).astype(BF16)
    tail_a = abuf[HIST_ROWS + q - 2:HIST_ROWS + q, :]
    nsc_ref[0] = tail_a
    abuf[HIST_ROWS - 2:HIST_ROWS, :] = tail_a

    u = xbc_ref[...]
    xbuf[HIST_ROWS:, :] = u
    cw = cw_ref[...]
    conv = (xbuf[HIST_ROWS - 3:HIST_ROWS - 3 + q, :] * cw[0:1]
            + xbuf[HIST_ROWS - 2:HIST_ROWS - 2 + q, :] * cw[1:2]
            + xbuf[HIST_ROWS - 1:HIST_ROWS - 1 + q, :] * cw[2:3]
            + u * cw[3:4] + cb_ref[...])
    tail_x = xbuf[HIST_ROWS + q - 3:HIST_ROWS + q, :]
    nxbc_ref[0] = tail_x
    xbuf[HIST_ROWS - 3:HIST_ROWS, :] = tail_x
    xbc = _silu(conv)
    xs = xbc[:, :D_INNER]
    bm = xbc[:, D_INNER:D_INNER + D_BC]
    cm = xbc[:, D_INNER + D_BC:].astype(BF16)

    dt = _softplus(dtr_ref[...] + dtb_ref[...])
    da = dt * (-jnp.exp(alog_ref[...]))
    row = lax.broadcasted_iota(jnp.int32, (q, LANES), 0)
    lane = lax.broadcasted_iota(jnp.int32, (q, LANES), 1)
    acum = da
    shift = 1
    while shift < q:
        acum = acum + jnp.where(row >= shift, pltpu.roll(acum, shift, axis=0), 0.0)
        shift *= 2

    pieces = jnp.concatenate(_split3(dt) + _split3(acum), axis=0)
    ex = jnp.dot(pieces, e_ref[...], preferred_element_type=F32)
    dt_x = ex[0:q] + ex[q:2 * q] + ex[2 * q:3 * q]
    ac_x = ex[3 * q:4 * q] + ex[4 * q:5 * q] + ex[5 * q:6 * q]
    last_x = ac_x[q - 1:q, :]

    xs_dt = xs * dt_x
    xs_dt_b = xs_dt.astype(BF16)
    xs_dec = (xs_dt * jnp.exp(last_x - ac_x)).astype(BF16)
    e_acum = jnp.exp(ac_x)
    decay = jnp.exp(last_x)

    acum_t = jnp.concatenate([acum, acum], axis=0).T
    lane_sq = lax.broadcasted_iota(jnp.int32, (LANES, LANES), 1)
    pair_rows = jnp.where(lane_sq < q, acum_t, pltpu.roll(acum_t, LANES - 1, axis=0))
    causal = row >= (lane & (q - 1))
    lo_half = lane < HEAD_DIM

    bm_t = bm.T.astype(BF16)
    bm_b = bm.astype(BF16)

    for g in range(GROUPS):
        c_g = cm[:, g * STATE:(g + 1) * STATE]
        b_g = bm_b[:, g * STATE:(g + 1) * STATE]
        b2 = jnp.concatenate([b_g, b_g], axis=0)
        cb2 = lax.dot_general(c_g, b2, (((1,), (1,)), ((), ())),
                              preferred_element_type=F32)
        cols = slice(g * NORM_GROUP, (g + 1) * NORM_GROUP)
        y_diag = []
        for jj in range(HEADS_PER_GROUP // 2):
            j = g * (HEADS_PER_GROUP // 2) + jj
            pc = slice(j * LANES, (j + 1) * LANES)
            seg = ac_x[:, pc] - pair_rows[2 * j:2 * j + 1, :]
            lmat = jnp.exp(jnp.where(causal, seg, -jnp.inf))
            m2 = (cb2 * lmat).astype(BF16)
            xp = xs_dt_b[:, pc]
            zero = jnp.zeros_like(xp)
            blk = jnp.concatenate([jnp.where(lo_half, xp, zero),
                                   jnp.where(lo_half, zero, xp)], axis=0)
            y_diag.append(jnp.dot(m2, blk, preferred_element_type=F32))
        y_diag = jnp.concatenate(y_diag, axis=1)
        st_g = st_ref[0, :, cols]
        y_off = jnp.dot(c_g, st_g.astype(BF16), preferred_element_type=F32)
        y_g = y_diag + y_off * e_acum[:, cols] + xs[:, cols] * dexp_ref[:, cols]
        st_ref[0, :, cols] = st_g * decay[:, cols] + jnp.dot(
            bm_t[g * STATE:(g + 1) * STATE, :], xs_dec[:, cols], preferred_element_type=F32)
        ug = y_g * _silu(z_ref[:, cols])
        ug = ug * lax.rsqrt(jnp.mean(ug * ug, axis=-1, keepdims=True) + EPS)
        y_ref[:, D_CONV + g * NORM_GROUP:D_CONV + (g + 1) * NORM_GROUP] = (
            ug * nw_ref[:, cols]).astype(BF16)


def _mixer(proj, dtr, sc_hist, xbc_hist, st0, scw, cw, cb, dtb, alog, dexp, nw, expand,
           *, prompt_chunks):
    m = proj.shape[0]
    nseq = st0.shape[0]
    q = CHUNK
    seq = lambda i: (jnp.maximum(i - (prompt_chunks - 1), 0), 0, 0)
    const = lambda i: (0, 0)
    return pl.pallas_call(
        functools.partial(_mixer_kernel, prompt_chunks=prompt_chunks),
        grid=(m // q,),
        in_specs=[
            pl.BlockSpec((q, D_XBC), lambda i: (i, 0)),
            pl.BlockSpec((q, D_INNER), lambda i: (i, D_XBC // D_INNER)),
            pl.BlockSpec((q, D_A), lambda i: (i, (D_XBC + D_INNER) // D_A)),
            pl.BlockSpec((q, LANES), lambda i: (i, 0)),
            pl.BlockSpec((1, 2, D_CONV), seq),
            pl.BlockSpec((1, 3, D_XBC), seq),
            pl.BlockSpec((1, STATE, D_INNER), seq),
            pl.BlockSpec((3, D_CONV), const),
            pl.BlockSpec((4, D_XBC), const),
            pl.BlockSpec((1, D_XBC), const),
            pl.BlockSpec((1, LANES), const),
            pl.BlockSpec((1, LANES), const),
            pl.BlockSpec((1, D_INNER), const),
            pl.BlockSpec((1, D_INNER), const),
            pl.BlockSpec((LANES, D_INNER), const),
        ],
        out_specs=[
            pl.BlockSpec((q, D_MIX), lambda i: (i, 0)),
            pl.BlockSpec((1, 2, D_CONV), seq),
            pl.BlockSpec((1, 3, D_XBC), seq),
            pl.BlockSpec((1, STATE, D_INNER), seq),
        ],
        out_shape=[
            jax.ShapeDtypeStruct((m, D_MIX), BF16),
            jax.ShapeDtypeStruct((nseq, 2, D_CONV), F32),
            jax.ShapeDtypeStruct((nseq, 3, D_XBC), F32),
            jax.ShapeDtypeStruct((nseq, STATE, D_INNER), F32),
        ],
        scratch_shapes=[pltpu.VMEM((HIST_ROWS + q, D_CONV), F32),
                        pltpu.VMEM((HIST_ROWS + q, D_XBC), F32)],
        compiler_params=pltpu.CompilerParams(
            dimension_semantics=("arbitrary",), vmem_limit_bytes=VMEM_LIMIT),
        name="mixer",
    )(proj, proj, proj, dtr, sc_hist, xbc_hist, st0, scw, cw, cb, dtb, alog, dexp, nw, expand)


def _pad_lanes(v):
    return jnp.pad(v, ((0, 0), (0, LANES - v.shape[1])))


def kernel(x_prompt, x_sample, state_sc_conv, state_ssd_conv, state_ssm, ffn1_norm, ffn1_w_gate, ffn1_w_up, ffn1_w_down, mix_norm, w_in_mix, sc_conv_w, ssd_conv_w, ssd_conv_b, ssd_dt_bias, ssd_A_log, ssd_D, ssd_norm, w_out_mix, ffn2_norm, ffn2_w_gate, ffn2_w_up, ffn2_w_down, final_norm):
    depth = ffn1_norm.shape[0]
    b0, t0, d = x_prompt.shape
    b1, t1, _ = x_sample.shape
    assert b0 == 1 and t0 % CHUNK == 0 and t1 == CHUNK
    m0, m1 = b0 * t0, b1 * t1
    m = m0 + m1
    tm = 512
    assert m % tm == 0
    nseq = b0 + b1

    x = jnp.concatenate([x_prompt.reshape(m0, d), x_sample.reshape(m1, d)], axis=0)

    expand = (lax.broadcasted_iota(jnp.int32, (LANES, D_INNER), 0)
              == lax.broadcasted_iota(jnp.int32, (LANES, D_INNER), 1) // HEAD_DIM).astype(BF16)
    fin = final_norm.reshape(1, d)

    new_sc, new_xbc, new_ssm = [], [], []
    for l in range(depth):
        x = _ffn(x, ffn1_norm[l].reshape(1, d), ffn1_w_gate[l].astype(BF16),
                 ffn1_w_up[l].astype(BF16), ffn1_w_down[l].astype(BF16), fin,
                 apply_final=False, tm=tm, tf=512)

        w_in = w_in_mix[l]
        w_perm = jnp.concatenate(
            [w_in[:, D_A + D_INNER:D_A + D_INNER + D_XBC], w_in[:, D_A:D_A + D_INNER], w_in[:, :D_A]],
            axis=1).astype(BF16)
        w_dt = _pad_lanes(w_in[:, D_A + D_INNER + D_XBC:]).astype(BF16)
        proj, dtr = _inproj(x, mix_norm[l].reshape(1, d), w_perm, w_dt, tm=tm, tn=1024)

        zeros = lambda *s: jnp.zeros(s, F32)
        sc_hist = jnp.concatenate([zeros(b0, 2, D_CONV), state_sc_conv[l]], axis=0)
        xbc_hist = jnp.concatenate([zeros(b0, 3, D_XBC), state_ssd_conv[l]], axis=0)
        st_t = state_ssm[l].transpose(0, 3, 1, 2).reshape(b1, STATE, D_INNER)
        st0 = jnp.concatenate([zeros(b0, STATE, D_INNER), st_t], axis=0)
        ymix, nsc, nxbc, nst = _mixer(
            proj, dtr, sc_hist, xbc_hist, st0,
            sc_conv_w[l], ssd_conv_w[l], ssd_conv_b[l].reshape(1, D_XBC),
            _pad_lanes(ssd_dt_bias[l].reshape(1, HEADS)), _pad_lanes(ssd_A_log[l].reshape(1, HEADS)),
            jnp.repeat(ssd_D[l], HEAD_DIM).reshape(1, D_INNER), ssd_norm[l].reshape(1, D_INNER),
            expand, prompt_chunks=m0 // CHUNK)
        new_sc.append(nsc)
        new_xbc.append(nxbc)
        new_ssm.append(nst.reshape(nseq, STATE, HEADS, HEAD_DIM).transpose(0, 2, 3, 1))

        x = _outproj(x, ymix, w_out_mix[l].astype(BF16), tm=tm, tn=1024)
        x = _ffn(x, ffn2_norm[l].reshape(1, d), ffn2_w_gate[l].astype(BF16),
                 ffn2_w_up[l].astype(BF16), ffn2_w_down[l].astype(BF16), fin,
                 apply_final=(l == depth - 1), tm=tm, tf=512)

    new_sc = jnp.stack(new_sc)
    new_xbc = jnp.stack(new_xbc)
    new_ssm = jnp.stack(new_ssm)
    return (x[:m0].reshape(b0, t0, d), x[m0:].reshape(b1, t1, d),
            new_sc[:, :b0], new_xbc[:, :b0], new_ssm[:, :b0],
            new_sc[:, b0:], new_xbc[:, b0:], new_ssm[:, b0:])
```
